```python
import jax, jax.numpy as jnp
from jax import lax
import numpy as np

D_MODEL = 1024
BATCH = 2
SEQ = 8192
DEPTH = 2

CHUNK = 64
LEFT_CHUNKS = 8
BAND = (LEFT_CHUNKS + 1) * CHUNK
HEAD_DIM = 64
N_HEADS_A = 8
N_HEADS_C = 8
WIDTH_A = N_HEADS_A * HEAD_DIM
WIDTH_C = N_HEADS_C * HEAD_DIM
POOL_WINDOWS = (2, 4, 8, 16)
N_POOL_GROUPS = 4
POOL_GROUP_DIM = 128
WIDTH_B = N_POOL_GROUPS * POOL_GROUP_DIM
REL_CLIP = 128
N_REL = 2 * REL_CLIP + 1
N_BRANCHES = 3
D_FF = 2816
Q_BLOCK = 128
N_MOD = 9
EPS = 1e-6
NEG_INF = -1e30

SPLIT_SIZES = (WIDTH_A, WIDTH_A, WIDTH_A, WIDTH_B, WIDTH_C, WIDTH_C, WIDTH_C, N_HEADS_C, N_BRANCHES * D_MODEL)
D_IN = sum(SPLIT_SIZES)
SPLIT_POINTS = tuple(int(p) for p in np.cumsum(SPLIT_SIZES)[:-1])

kernel_name = "hybrid_chunk_causal_encoder"


def rms_norm(x, gain):
    xf = x.astype(jnp.float32)
    y = xf * lax.rsqrt(jnp.mean(xf * xf, axis=-1, keepdims=True) + EPS)
    return (y * gain.astype(jnp.float32)).astype(x.dtype)


def modulate(h, shift, scale):
    return h * (1 + scale[:, None, :]) + shift[:, None, :]


def swiglu(h, w_gate, w_up, w_down):
    return (jax.nn.silu(h @ w_gate) * (h @ w_up)) @ w_down


def chunked_relpos_attention(q, k, v, gq, gk, rel_bias):
    B, S, H, dh = q.shape
    nc = S // CHUNK
    qf = rms_norm(q, gq).astype(jnp.float32) * (dh ** -0.5)
    kf = rms_norm(k, gk).astype(jnp.float32)
    qc = qf.reshape(B, nc, CHUNK, H, dh)
    pad = ((0, 0), (LEFT_CHUNKS * CHUNK, 0), (0, 0), (0, 0))
    kp = jnp.pad(kf, pad).reshape(B, nc + LEFT_CHUNKS, CHUNK, H, dh)
    vp = jnp.pad(v, pad).reshape(B, nc + LEFT_CHUNKS, CHUNK, H, dh)
    band_idx = np.arange(nc)[:, None] + np.arange(LEFT_CHUNKS + 1)[None, :]
    kb = kp[:, band_idx].reshape(B, nc, BAND, H, dh)
    vb = vp[:, band_idx].reshape(B, nc, BAND, H, dh)
    logits = jnp.einsum('bnqhd,bnkhd->bhnqk', qc, kb)
    rel = np.arange(CHUNK)[:, None] + LEFT_CHUNKS * CHUNK - np.arange(BAND)[None, :]
    rel_idx = np.clip(rel, -REL_CLIP, REL_CLIP) + REL_CLIP
    bias = rel_bias[:, rel_idx].astype(jnp.float32)
    valid = np.repeat(band_idx >= LEFT_CHUNKS, CHUNK, axis=1)
    logits = jnp.where(valid[None, None, :, None, :], logits + bias[None, :, None], NEG_INF)
    p = jax.nn.softmax(logits, axis=-1).astype(v.dtype)
    out = jnp.einsum('bhnqk,bnkhd->bnqhd', p, vb)
    return out.reshape(B, S, H * dh)


def multiscale_pool(u, w_pool, pool_scale):
    B, S, _ = u.shape
    uf = u.astype(jnp.float32)
    cs = jnp.concatenate([jnp.zeros((B, 1, WIDTH_B), jnp.float32), jnp.cumsum(uf, axis=1)], axis=1)
    t = jnp.arange(S, dtype=jnp.float32)
    diffs = []
    for g, w in enumerate(POOL_WINDOWS):
        sl = slice(g * POOL_GROUP_DIM, (g + 1) * POOL_GROUP_DIM)
        csg = cs[:, :, sl]
        upper = csg[:, 1:]
        lower = jnp.pad(csg[:, :S + 1 - w], ((0, 0), (w - 1, 0), (0, 0)))
        count = jnp.minimum(t + 1, float(w))[None, :, None]
        diffs.append((upper - lower) / count - uf[:, :, sl])
    d = jnp.stack(diffs, axis=2).astype(u.dtype)
    y = jnp.einsum('bsgc,gce->bsge', d, w_pool).reshape(B, S, WIDTH_B)
    return y * pool_scale


def forgetting_attention(q, k, v, gq, gk, f_logit):
    B, S, H, dh = q.shape
    qf = rms_norm(q, gq).astype(jnp.float32) * (dh ** -0.5)
    kf = rms_norm(k, gk).astype(jnp.float32)
    cum_logf = jnp.cumsum(jax.nn.log_sigmoid(f_logit.astype(jnp.float32)), axis=1)
    cum_k = cum_logf.transpose(0, 2, 1)
    kpos = jnp.arange(S)

    def block(i):
        start = i * Q_BLOCK
        qb = lax.dynamic_slice_in_dim(qf, start, Q_BLOCK, axis=1)
        cq = lax.dynamic_slice_in_dim(cum_k, start, Q_BLOCK, axis=2)
        logits = jnp.einsum('bqhd,bkhd->bhqk', qb, kf) + (cq[..., None] - cum_k[:, :, None, :])
        qpos = start + jnp.arange(Q_BLOCK)
        mask = kpos[None, :] <= qpos[:, None]
        logits = jnp.where(mask[None, None], logits, NEG_INF)
        p = jax.nn.softmax(logits, axis=-1).astype(v.dtype)
        return jnp.einsum('bhqk,bkhd->bqhd', p, v)

    out = lax.map(block, jnp.arange(S // Q_BLOCK))
    return out.transpose(1, 0, 2, 3, 4).reshape(B, S, H * dh)


def hybrid_mixer(h, w_in, qk_gain, rel_bias, forget_bias, w_pool, pool_scale, w_branch, w_out):
    B, S, _ = h.shape
    proj = h @ w_in
    qa, ka, va, ub, qc, kc, vc, f_lin, g_lin = jnp.split(proj, SPLIT_POINTS, axis=-1)
    ha = lambda t: t.reshape(B, S, N_HEADS_A, HEAD_DIM)
    hc = lambda t: t.reshape(B, S, N_HEADS_C, HEAD_DIM)
    ya = chunked_relpos_attention(ha(qa), ha(ka), ha(va), qk_gain[0], qk_gain[1], rel_bias)
    yb = multiscale_pool(ub, w_pool, pool_scale)
    yc = forgetting_attention(hc(qc), hc(kc), hc(vc), qk_gain[2], qk_gain[3], f_lin + forget_bias)
    ys = jnp.stack([ya, yb, yc], axis=2)
    branch = jnp.einsum('bsnw,nwd->bsnd', ys, w_branch)
    gates = jax.nn.sigmoid(g_lin.reshape(B, S, N_BRANCHES, D_MODEL))
    merged = jnp.sum(gates * branch, axis=2)
    return merged @ w_out


def setup_inputs(seed: int = 0) -> dict:
    key = jax.random.key(seed)
    ks = jax.random.split(key, 18)
    nrm = lambda k, shape, s: jax.random.normal(k, shape, jnp.float32) * s
    return {
        "x": nrm(ks[0], (BATCH, SEQ, D_MODEL), 1.0),
        "c": nrm(ks[1], (BATCH, D_MODEL), 1.0),
        "w_ada": nrm(ks[2], (DEPTH, D_MODEL, N_MOD * D_MODEL), 0.5 * D_MODEL ** -0.5),
        "b_ada": nrm(ks[3], (DEPTH, N_MOD * D_MODEL), 0.02),
        "norm_gain": 1.0 + nrm(ks[4], (DEPTH, 3, D_MODEL), 0.05),
        "ffn_w_gate": nrm(ks[5], (DEPTH, 2, D_MODEL, D_FF), D_MODEL ** -0.5),
        "ffn_w_up": nrm(ks[6], (DEPTH, 2, D_MODEL, D_FF), D_MODEL ** -0.5),
        "ffn_w_down": nrm(ks[7], (DEPTH, 2, D_FF, D_MODEL), D_FF ** -0.5),
        "w_in": nrm(ks[8], (DEPTH, D_MODEL, D_IN), D_MODEL ** -0.5),
        "qk_gain": 1.0 + nrm(ks[9], (DEPTH, 4, HEAD_DIM), 0.05),
        "rel_bias": nrm(ks[10], (DEPTH, N_HEADS_A, N_REL), 0.5),
        "forget_bias": 1.0 + 3.0 * jax.random.uniform(ks[11], (DEPTH, N_HEADS_C), jnp.float32),
        "w_pool": nrm(ks[12], (DEPTH, N_POOL_GROUPS, POOL_GROUP_DIM, POOL_GROUP_DIM), POOL_GROUP_DIM ** -0.5),
        "pool_scale": 1.0 + nrm(ks[13], (DEPTH, WIDTH_B), 0.05),
        "w_branch": nrm(ks[14], (DEPTH, N_BRANCHES, WIDTH_A, D_MODEL), WIDTH_A ** -0.5),
        "w_out": nrm(ks[15], (DEPTH, D_MODEL, D_MODEL), D_MODEL ** -0.5),
    }


def reference(x, c, w_ada, b_ada, norm_gain, ffn_w_gate, ffn_w_up, ffn_w_down, w_in, qk_gain,
              rel_bias, forget_bias, w_pool, pool_scale, w_branch, w_out):
    B = x.shape[0]
    c_act = jax.nn.silu(c)
    for l in range(DEPTH):
        mod = (c_act @ w_ada[l] + b_ada[l]).reshape(B, N_MOD, D_MODEL)
        h = modulate(rms_norm(x, norm_gain[l, 0]), mod[:, 0], mod[:, 1])
        x = x + 0.5 * mod[:, 2][:, None, :] * swiglu(h, ffn_w_gate[l, 0], ffn_w_up[l, 0], ffn_w_down[l, 0])
        h = modulate(rms_norm(x, norm_gain[l, 1]), mod[:, 3], mod[:, 4])
        x = x + mod[:, 5][:, None, :] * hybrid_mixer(h, w_in[l], qk_gain[l], rel_bias[l], forget_bias[l],
                                                      w_pool[l], pool_scale[l], w_branch[l], w_out[l])
        h = modulate(rms_norm(x, norm_gain[l, 2]), mod[:, 6], mod[:, 7])
        x = x + 0.5 * mod[:, 8][:, None, :] * swiglu(h, ffn_w_gate[l, 1], ffn_w_up[l, 1], ffn_w_down[l, 1])
    return x
```

```python
import functools

import jax
import jax.numpy as jnp
import numpy as np
from jax import lax
from jax.experimental import pallas as pl
from jax.experimental.pallas import tpu as pltpu

CHUNK = 64
LEFT_CHUNKS = 8
LEFT = LEFT_CHUNKS * CHUNK
HEAD_DIM = 64
N_HEADS = 8
WIDTH = N_HEADS * HEAD_DIM
POOL_WINDOWS = (2, 4, 8, 16)
POOL_GROUP_DIM = 128
REL_CLIP = 128
N_BRANCHES = 3
N_MOD = 9
EPS = 1e-6
NEG_INF = -1e30

LANES = 128
VMEM_LIMIT_BYTES = 56 * 1024 * 1024

TOKEN_TILE = 512
FF_CHUNK = 256
MOD_TILE = 1536
ATTN_A_ROWS = 256
ATTN_A_KEYS = LEFT + ATTN_A_ROWS
FOX_TQ = 512
FOX_TK = 512
POOL_HALO = 16

_BF16 = jnp.bfloat16
_F32 = jnp.float32
_NT = (((1,), (1,)), ((), ()))


def _params(*sem):
    return pltpu.CompilerParams(dimension_semantics=sem, vmem_limit_bytes=VMEM_LIMIT_BYTES)


def _resident(shape, index_map):
    return pl.BlockSpec(shape, index_map, pipeline_mode=pl.Buffered(1))


def _norm_mod(x, gain, shift, scale):
    ms = jnp.mean(x * x, axis=-1, keepdims=True)
    y = x * lax.rsqrt(ms + EPS)
    return (y * gain) * (1.0 + scale) + shift


def _sigmoid(x):
    return 1.0 / (1.0 + jnp.exp(-x))


def _mod_kernel(ct_ref, w_ref, b_ref, o_ref):
    c = ct_ref[...]
    c_act = c * _sigmoid(c)
    w = w_ref[0]
    for b in range(ct_ref.shape[1]):
        row = jnp.sum(c_act[:, b:b + 1] * w, axis=0, keepdims=True)
        o_ref[0, b:b + 1, :] = row + b_ref[0]


def _adaln(c, w_ada, b_ada):
    depth, d, n = w_ada.shape
    bsz = c.shape[0]
    return pl.pallas_call(
        _mod_kernel,
        grid=(depth, n // MOD_TILE),
        in_specs=[
            pl.BlockSpec((d, bsz), lambda l, j: (0, 0)),
            pl.BlockSpec((1, d, MOD_TILE), lambda l, j: (l, 0, j)),
            pl.BlockSpec((1, 1, MOD_TILE), lambda l, j: (l, 0, j)),
        ],
        out_specs=pl.BlockSpec((1, bsz, MOD_TILE), lambda l, j: (l, 0, j)),
        out_shape=jax.ShapeDtypeStruct((depth, bsz, n), _F32),
        compiler_params=_params("arbitrary", "arbitrary"),
        name="adaln_mod",
    )(c.T, w_ada, b_ada.reshape(depth, 1, n))


def _ffn_kernel(x_ref, mod_ref, gain_ref, wg_ref, wu_ref, wd_ref, o_ref, *, mod_base):
    x = x_ref[...]
    shift = mod_ref[0, mod_base:mod_base + 1, :]
    scale = mod_ref[0, mod_base + 1:mod_base + 2, :]
    gate = mod_ref[0, mod_base + 2:mod_base + 3, :]
    h = _norm_mod(x, gain_ref[...], shift, scale).astype(_BF16)
    d_ff = wg_ref.shape[1]
    acc = jnp.zeros(x.shape, _F32)
    for c in range(d_ff // FF_CHUNK):
        cols = slice(c * FF_CHUNK, (c + 1) * FF_CHUNK)
        g = jnp.dot(h, wg_ref[:, cols], preferred_element_type=_F32)
        u = jnp.dot(h, wu_ref[:, cols], preferred_element_type=_F32)
        a = (g * _sigmoid(g)) * u
        acc = acc + jnp.dot(a.astype(_BF16), wd_ref[cols, :], preferred_element_type=_F32)
    o_ref[...] = x + (0.5 * gate) * acc


def _ffn(x2, mod_l, gain, wg, wu, wd, *, mod_base, seq):
    n_tok, d = x2.shape
    d_ff = wg.shape[1]
    tiles_per_seq = seq // TOKEN_TILE
    return pl.pallas_call(
        functools.partial(_ffn_kernel, mod_base=mod_base),
        grid=(n_tok // TOKEN_TILE,),
        in_specs=[
            pl.BlockSpec((TOKEN_TILE, d), lambda i: (i, 0)),
            pl.BlockSpec((1, N_MOD, d), lambda i: (i // tiles_per_seq, 0, 0)),
            _resident((1, d), lambda i: (0, 0)),
            _resident((d, d_ff), lambda i: (0, 0)),
            _resident((d, d_ff), lambda i: (0, 0)),
            _resident((d_ff, d), lambda i: (0, 0)),
        ],
        out_specs=pl.BlockSpec((TOKEN_TILE, d), lambda i: (i, 0)),
        out_shape=jax.ShapeDtypeStruct((n_tok, d), _F32),
        compiler_params=_params("arbitrary"),
        name="swiglu_ffn",
    )(x2, mod_l, gain, wg, wu, wd)


def _log_sigmoid(x):
    return -(jnp.maximum(-x, 0.0) + jnp.log1p(jnp.exp(-jnp.abs(x))))


def _split3(x):
    p0 = x.astype(_BF16)
    r = x - p0.astype(_F32)
    p1 = r.astype(_BF16)
    p2 = (r - p1.astype(_F32)).astype(_BF16)
    return p0, p1, p2


def _inproj_kernel(x_ref, mod_ref, gain_ref, wmain_ref, wf_ref, fb_ref, headsum_ref, qkg_ref,
                   wpool_ref, pscale_ref, kpad_in, vpad_in,
                   qa_ref, ka_ref, va_ref, yb_ref, qc_ref, kc_ref, vc_ref, cum_ref,
                   ext_ref, carry_ref, *, tiles_per_seq):
    del kpad_in, vpad_in
    tm = x_ref.shape[0]
    t = pl.program_id(0) % tiles_per_seq

    @pl.when(t == 0)
    def _():
        carry_ref[...] = jnp.zeros_like(carry_ref)
        ext_ref[0:POOL_HALO, :] = jnp.zeros((POOL_HALO, WIDTH), _F32)

    h = _norm_mod(x_ref[...], gain_ref[...], mod_ref[0, 3:4, :], mod_ref[0, 4:5, :]).astype(_BF16)

    def proj(k):
        return jnp.dot(h, wmain_ref[:, k * WIDTH:(k + 1) * WIDTH], preferred_element_type=_F32)

    def head_norm(y, row):
        ssum = jnp.dot((y * y).astype(_BF16), headsum_ref[...], preferred_element_type=_F32)
        return (y * lax.rsqrt(ssum * (1.0 / HEAD_DIM) + EPS)) * qkg_ref[row:row + 1, :]

    qa_ref[...] = head_norm(proj(0), 0).astype(_BF16)
    ka_ref[0] = head_norm(proj(1), 1).astype(_BF16)
    va_ref[0] = proj(2).astype(_BF16)
    qc_ref[...] = head_norm(proj(4), 2).astype(_BF16)
    kc_ref[...] = head_norm(proj(5), 3).astype(_BF16)
    vc_ref[...] = proj(6).astype(_BF16)

    u = proj(3)
    ext_ref[POOL_HALO:POOL_HALO + tm, :] = u
    pos = t * tm + lax.broadcasted_iota(jnp.int32, (tm, 1), 0)
    for g, w in enumerate(POOL_WINDOWS):
        lanes = slice(g * POOL_GROUP_DIM, (g + 1) * POOL_GROUP_DIM)
        ug = u[:, lanes]
        win = ug
        for k in range(1, w):
            win = win + ext_ref[POOL_HALO - k:POOL_HALO - k + tm, lanes]
        count = jnp.minimum(pos + 1, w).astype(_F32)
        diff = win / count - ug
        y = jnp.dot(diff.astype(_BF16), wpool_ref[g], preferred_element_type=_F32)
        yb_ref[:, lanes] = (y * pscale_ref[:, lanes]).astype(_BF16)
    ext_ref[0:POOL_HALO, :] = ext_ref[tm:tm + POOL_HALO, :]

    f_lin = jnp.dot(h, wf_ref[...], preferred_element_type=_F32)
    logf = _log_sigmoid(f_lin + fb_ref[...])
    row = lax.broadcasted_iota(jnp.int32, (tm, tm), 0)
    col = lax.broadcasted_iota(jnp.int32, (tm, tm), 1)
    tri = (row >= col).astype(_BF16)
    cum = carry_ref[0:1, :]
    for piece in _split3(logf):
        cum = cum + jnp.dot(tri, piece, preferred_element_type=_F32)
    cum_ref[...] = cum
    carry_ref[0:1, :] = cum[tm - 1:tm, :]


def _inproj(x2, mod_l, gain, wmain, wf, fb, headsum, qkg, wpool, pscale, *, bsz, seq):
    n_tok, d = x2.shape
    tm = TOKEN_TILE
    tps = seq // tm
    kpad = jnp.zeros((bsz, seq + LEFT, WIDTH), _BF16)
    vpad = jnp.zeros((bsz, seq + LEFT, WIDTH), _BF16)
    tok = pl.BlockSpec((tm, WIDTH), lambda i: (i, 0))
    padded = pl.BlockSpec((1, tm, WIDTH), lambda i: (i // tps, i % tps + LEFT // tm, 0))
    flat = jax.ShapeDtypeStruct((n_tok, WIDTH), _BF16)
    pad_shape = jax.ShapeDtypeStruct((bsz, seq + LEFT, WIDTH), _BF16)
    return pl.pallas_call(
        functools.partial(_inproj_kernel, tiles_per_seq=tps),
        grid=(n_tok // tm,),
        in_specs=[
            pl.BlockSpec((tm, d), lambda i: (i, 0)),
            pl.BlockSpec((1, N_MOD, d), lambda i: (i // tps, 0, 0)),
            _resident((1, d), lambda i: (0, 0)),
            _resident(wmain.shape, lambda i: (0, 0)),
            _resident(wf.shape, lambda i: (0, 0)),
            _resident(fb.shape, lambda i: (0, 0)),
            _resident(headsum.shape, lambda i: (0, 0)),
            _resident(qkg.shape, lambda i: (0, 0)),
            _resident(wpool.shape, lambda i: (0, 0, 0)),
            _resident(pscale.shape, lambda i: (0, 0)),
            pl.BlockSpec(memory_space=pl.ANY),
            pl.BlockSpec(memory_space=pl.ANY),
        ],
        out_specs=[tok, padded, padded, tok, tok, tok, tok,
                   pl.BlockSpec((tm, LANES), lambda i: (i, 0))],
        out_shape=[flat, pad_shape, pad_shape, flat, flat, flat, flat,
                   jax.ShapeDtypeStruct((n_tok, LANES), _F32)],
        scratch_shapes=[pltpu.VMEM((POOL_HALO + tm, WIDTH), _F32), pltpu.VMEM((8, LANES), _F32)],
        input_output_aliases={10: 1, 11: 2},
        compiler_params=_params("arbitrary"),
        name="mixer_inproj",
    )(x2, mod_l, gain, wmain, wf, fb, headsum, qkg, wpool, pscale, kpad, vpad)


def _attn_a_kernel(q_ref, k_ref, v_ref, bias_ref, o_ref):
    r_rows = q_ref.shape[1]
    n_keys = bias_ref.shape[2]
    start = pl.multiple_of(pl.program_id(1) * r_rows, r_rows)
    lane = lax.broadcasted_iota(jnp.int32, (1, LANES), 1)
    low_half = lane < HEAD_DIM
    key_row = start + lax.broadcasted_iota(jnp.int32, (1, n_keys), 1)
    real = key_row >= LEFT
    for pair in range(N_HEADS // 2):
        lanes = slice(pair * LANES, (pair + 1) * LANES)
        q2 = q_ref[0, :, lanes]
        k2 = k_ref[0, pl.ds(start, n_keys), lanes]
        v2 = v_ref[0, pl.ds(start, n_keys), lanes]
        outs = []
        for hh in range(2):
            mine = low_half if hh == 0 else jnp.logical_not(low_half)
            qm = jnp.where(mine, q2, jnp.zeros_like(q2))
            s = lax.dot_general(qm, k2, _NT, preferred_element_type=_F32)
            s = jnp.where(real, s + bias_ref[2 * pair + hh], NEG_INF)
            m = jnp.max(s, axis=1, keepdims=True)
            p = jnp.exp(s - m)
            denom = jnp.sum(p, axis=1, keepdims=True)
            o = jnp.dot(p.astype(_BF16), v2, preferred_element_type=_F32)
            outs.append(o / denom)
        o_ref[0, :, lanes] = jnp.where(low_half, outs[0], outs[1]).astype(_BF16)


def _attn_a(qa3, kpad, vpad, bias):
    bsz, seq, _ = qa3.shape
    r_rows = ATTN_A_ROWS
    return pl.pallas_call(
        _attn_a_kernel,
        grid=(bsz, seq // r_rows),
        in_specs=[
            pl.BlockSpec((1, r_rows, WIDTH), lambda b, i: (b, i, 0)),
            pl.BlockSpec((1, seq + LEFT, WIDTH), lambda b, i: (b, 0, 0), pipeline_mode=pl.Buffered(1)),
            pl.BlockSpec((1, seq + LEFT, WIDTH), lambda b, i: (b, 0, 0), pipeline_mode=pl.Buffered(1)),
            _resident(bias.shape, lambda b, i: (0, 0, 0)),
        ],
        out_specs=pl.BlockSpec((1, r_rows, WIDTH), lambda b, i: (b, i, 0)),
        out_shape=jax.ShapeDtypeStruct((bsz, seq, WIDTH), _BF16),
        compiler_params=_params("arbitrary", "arbitrary"),
        name="chunk_attention",
    )(qa3, kpad, vpad, bias)


def _band_bias(rel_bias_l):
    r = np.arange(ATTN_A_ROWS)[:, None]
    p = np.arange(ATTN_A_KEYS)[None, :]
    rel_idx = np.clip(r - p + LEFT, -REL_CLIP, REL_CLIP) + REL_CLIP
    in_band = (p // CHUNK >= r // CHUNK) & (p // CHUNK <= r // CHUNK + LEFT_CHUNKS)
    return jnp.where(in_band[None], rel_bias_l[:, rel_idx], NEG_INF)


def _fox_kernel(q_ref, k_ref, v_ref, ccol_ref, crow_ref, o_ref, m_ref, l_ref, acc_ref):
    tq = q_ref.shape[1]
    tk = FOX_TK
    i = pl.program_id(2)
    q2 = q_ref[0]
    lane = lax.broadcasted_iota(jnp.int32, (1, LANES), 1)
    low_half = lane < HEAD_DIM
    m_ref[...] = jnp.full(m_ref.shape, NEG_INF, _F32)
    l_ref[...] = jnp.zeros(l_ref.shape, _F32)
    acc_ref[...] = jnp.zeros(acc_ref.shape, _F32)

    def block(kstart, diagonal):
        k2 = k_ref[0, pl.ds(kstart, tk), :]
        v2 = v_ref[0, pl.ds(kstart, tk), :]
        for hh in range(2):
            mine = low_half if hh == 0 else jnp.logical_not(low_half)
            qm = jnp.where(mine, q2, jnp.zeros_like(q2))
            s = lax.dot_general(qm, k2, _NT, preferred_element_type=_F32)
            cq = ccol_ref[0, 0, :, hh:hh + 1]
            ck = crow_ref[0, 0, hh:hh + 1, pl.ds(kstart, tk)]
            s = s + (cq - ck)
            if diagonal:
                row = lax.broadcasted_iota(jnp.int32, (tq, tk), 0)
                col = lax.broadcasted_iota(jnp.int32, (tq, tk), 1)
                s = jnp.where(row >= col, s, NEG_INF)
            m_prev = m_ref[hh]
            m_new = jnp.maximum(m_prev, jnp.max(s, axis=1, keepdims=True))
            alpha = jnp.exp(m_prev - m_new)
            p = jnp.exp(s - m_new)
            l_ref[hh] = alpha * l_ref[hh] + jnp.sum(p, axis=1, keepdims=True)
            acc_ref[hh] = alpha * acc_ref[hh] + jnp.dot(p.astype(_BF16), v2, preferred_element_type=_F32)
            m_ref[hh] = m_new

    def body(kk, carry):
        block(pl.multiple_of(kk * tk, tk), False)
        return carry

    lax.fori_loop(0, i, body, 0)
    block(pl.multiple_of(i * tq, tq), True)
    out = jnp.where(low_half, acc_ref[0] / l_ref[0], acc_ref[1] / l_ref[1])
    o_ref[0] = out.astype(_BF16)


def _fox(qc3, kc3, vc3, ccol4, crow4):
    bsz, seq, _ = qc3.shape
    tq = FOX_TQ
    pairs = N_HEADS // 2
    return pl.pallas_call(
        _fox_kernel,
        grid=(bsz, pairs, seq // tq),
        in_specs=[
            pl.BlockSpec((1, tq, LANES), lambda b, j, i: (b, i, j)),
            pl.BlockSpec((1, seq, LANES), lambda b, j, i: (b, 0, j)),
            pl.BlockSpec((1, seq, LANES), lambda b, j, i: (b, 0, j)),
            pl.BlockSpec((1, 1, tq, 2), lambda b, j, i: (b, j, i, 0)),
            pl.BlockSpec((1, 1, 2, seq), lambda b, j, i: (b, j, 0, 0)),
        ],
        out_specs=pl.BlockSpec((1, tq, LANES), lambda b, j, i: (b, i, j)),
        out_shape=jax.ShapeDtypeStruct((bsz, seq, WIDTH), _BF16),
        scratch_shapes=[pltpu.VMEM((2, tq, 1), _F32), pltpu.VMEM((2, tq, 1), _F32),
                        pltpu.VMEM((2, tq, LANES), _F32)],
        compiler_params=_params("arbitrary", "arbitrary", "arbitrary"),
        name="forgetting_attention",
    )(qc3, kc3, vc3, ccol4, crow4)


def _merge_kernel(x_ref, mod_ref, gain_ref, ya_ref, yb_ref, yc_ref, wgate_ref, wbranch_ref, wout_ref, o_ref):
    x = x_ref[...]
    d = x.shape[1]
    h = _norm_mod(x, gain_ref[...], mod_ref[0, 3:4, :], mod_ref[0, 4:5, :]).astype(_BF16)
    merged = jnp.zeros(x.shape, _F32)
    for n, y_ref in enumerate((ya_ref, yb_ref, yc_ref)):
        g = jnp.dot(h, wgate_ref[:, n * d:(n + 1) * d], preferred_element_type=_F32)
        branch = jnp.dot(y_ref[...], wbranch_ref[n], preferred_element_type=_F32)
        merged = merged + _sigmoid(g) * branch
    y = jnp.dot(merged.astype(_BF16), wout_ref[...], preferred_element_type=_F32)
    o_ref[...] = x + mod_ref[0, 5:6, :] * y


def _merge(x2, mod_l, gain, ya, yb, yc, wgate, wbranch, wout, *, seq):
    n_tok, d = x2.shape
    tm = TOKEN_TILE
    tps = seq // tm
    tok = pl.BlockSpec((tm, WIDTH), lambda i: (i, 0))
    return pl.pallas_call(
        _merge_kernel,
        grid=(n_tok // tm,),
        in_specs=[
            pl.BlockSpec((tm, d), lambda i: (i, 0)),
            pl.BlockSpec((1, N_MOD, d), lambda i: (i // tps, 0, 0)),
            _resident((1, d), lambda i: (0, 0)),
            tok, tok, tok,
            _resident(wgate.shape, lambda i: (0, 0)),
            _resident(wbranch.shape, lambda i: (0, 0, 0)),
            _resident(wout.shape, lambda i: (0, 0)),
        ],
        out_specs=pl.BlockSpec((tm, d), lambda i: (i, 0)),
        out_shape=jax.ShapeDtypeStruct((n_tok, d), _F32),
        compiler_params=_params("arbitrary"),
        name="gated_merge",
    )(x2, mod_l, gain, ya, yb, yc, wgate, wbranch, wout)


def kernel(x, c, w_ada, b_ada, norm_gain, ffn_w_gate, ffn_w_up, ffn_w_down, w_in, qk_gain,
           rel_bias, forget_bias, w_pool, pool_scale, w_branch, w_out):
    bsz, seq, d = x.shape
    depth = w_ada.shape[0]
    assert seq % TOKEN_TILE == 0 and seq % FOX_TQ == 0 and seq % ATTN_A_ROWS == 0
    assert LEFT % TOKEN_TILE == 0 and LEFT % ATTN_A_ROWS == 0 and FOX_TQ == FOX_TK
    assert d % LANES == 0 and w_in.shape[2] == 7 * WIDTH + N_HEADS + N_BRANCHES * d
    n_tok = bsz * seq

    mod = _adaln(c, w_ada, b_ada).reshape(depth, bsz, N_MOD, d)

    head_id = np.arange(WIDTH) // HEAD_DIM
    headsum = jnp.asarray(head_id[:, None] == head_id[None, :], _BF16)
    main_cols = 7 * WIDTH

    x2 = x.reshape(n_tok, d)
    for l in range(depth):
        gains = norm_gain[l].reshape(3, 1, d)
        x2 = _ffn(x2, mod[l], gains[0], ffn_w_gate[l, 0].astype(_BF16), ffn_w_up[l, 0].astype(_BF16),
                  ffn_w_down[l, 0].astype(_BF16), mod_base=0, seq=seq)

        wmain = w_in[l, :, :main_cols].astype(_BF16)
        wf = jnp.pad(w_in[l, :, main_cols:main_cols + N_HEADS], ((0, 0), (0, LANES - N_HEADS))).astype(_BF16)
        wgate = w_in[l, :, main_cols + N_HEADS:].astype(_BF16)
        fb = jnp.pad(forget_bias[l], (0, LANES - N_HEADS)).reshape(1, LANES)
        q_scale = HEAD_DIM ** -0.5
        qkg = jnp.stack([jnp.tile(qk_gain[l, 0], N_HEADS) * q_scale, jnp.tile(qk_gain[l, 1], N_HEADS),
                         jnp.tile(qk_gain[l, 2], N_HEADS) * q_scale, jnp.tile(qk_gain[l, 3], N_HEADS)])
        qa, kpad, vpad, yb, qc, kc, vc, cum = _inproj(
            x2, mod[l], gains[1], wmain, wf, fb, headsum, qkg, w_pool[l].astype(_BF16),
            pool_scale[l].reshape(1, WIDTH), bsz=bsz, seq=seq)

        ya = _attn_a(qa.reshape(bsz, seq, WIDTH), kpad, vpad, _band_bias(rel_bias[l]))

        cum8 = cum.reshape(bsz, seq, LANES)[:, :, :N_HEADS]
        ccol4 = cum8.reshape(bsz, seq, N_HEADS // 2, 2).transpose(0, 2, 1, 3)
        crow4 = cum8.transpose(0, 2, 1).reshape(bsz, N_HEADS // 2, 2, seq)
        yc = _fox(qc.reshape(bsz, seq, WIDTH), kc.reshape(bsz, seq, WIDTH), vc.reshape(bsz, seq, WIDTH),
                  ccol4, crow4)

        x2 = _merge(x2, mod[l], gains[1], ya.reshape(n_tok, WIDTH), yb, yc.reshape(n_tok, WIDTH),
                    wgate, w_branch[l].astype(_BF16), w_out[l].astype(_BF16), seq=seq)

        x2 = _ffn(x2, mod[l], gains[2], ffn_w_gate[l, 1].astype(_BF16), ffn_w_up[l, 1].astype(_BF16),
                  ffn_w_down[l, 1].astype(_BF16), mod_base=6, seq=seq)
    return x2.reshape(bsz, seq, d)
```

```python
import functools

import jax
import jax.numpy as jnp
import numpy as np
from jax import lax
from jax.experimental import pallas as pl
from jax.experimental.pallas import tpu as pltpu

CHUNK = 64
LEFT_CHUNKS = 8
LEFT = LEFT_CHUNKS * CHUNK
HEAD_DIM = 64
N_HEADS = 8
WIDTH = N_HEADS * HEAD_DIM
POOL_WINDOWS = (2, 4, 8, 16)
POOL_GROUP_DIM = 128
REL_CLIP = 128
N_BRANCHES = 3
N_MOD = 9
EPS = 1e-6
NEG_INF = -1e30
LOG2E = 1.4426950408889634
FOX_SKIP_NATS = 64.0

LANES = 128
VMEM_LIMIT_BYTES = 56 * 1024 * 1024

TOKEN_TILE = 512
FF_CHUNK = 256
MOD_TILE = 1536
ATTN_A_ROWS = 256
ATTN_A_KEYS = LEFT + ATTN_A_ROWS
FOX_TQ = 512
FOX_TK = 512
POOL_HALO = 16

_BF16 = jnp.bfloat16
_F32 = jnp.float32
_NT = (((1,), (1,)), ((), ()))


def _params(*sem):
    return pltpu.CompilerParams(dimension_semantics=sem, vmem_limit_bytes=VMEM_LIMIT_BYTES)


def _resident(shape, index_map):
    return pl.BlockSpec(shape, index_map, pipeline_mode=pl.Buffered(1))


def _norm_mod(x, gain, shift, scale):
    ms = jnp.mean(x * x, axis=-1, keepdims=True)
    y = x * lax.rsqrt(ms + EPS)
    return (y * gain) * (1.0 + scale) + shift


def _sigmoid(x):
    return 1.0 / (1.0 + jnp.exp(-x))


def _mod_kernel(ct_ref, w_ref, b_ref, o_ref):
    c = ct_ref[...]
    c_act = c * _sigmoid(c)
    w = w_ref[0]
    for b in range(ct_ref.shape[1]):
        row = jnp.sum(c_act[:, b:b + 1] * w, axis=0, keepdims=True)
        o_ref[0, b:b + 1, :] = row + b_ref[0]


def _adaln(c, w_ada, b_ada):
    depth, d, n = w_ada.shape
    bsz = c.shape[0]
    return pl.pallas_call(
        _mod_kernel,
        grid=(depth, n // MOD_TILE),
        in_specs=[
            pl.BlockSpec((d, bsz), lambda l, j: (0, 0)),
            pl.BlockSpec((1, d, MOD_TILE), lambda l, j: (l, 0, j)),
            pl.BlockSpec((1, 1, MOD_TILE), lambda l, j: (l, 0, j)),
        ],
        out_specs=pl.BlockSpec((1, bsz, MOD_TILE), lambda l, j: (l, 0, j)),
        out_shape=jax.ShapeDtypeStruct((depth, bsz, n), _F32),
        compiler_params=_params("arbitrary", "arbitrary"),
        name="adaln_mod",
    )(c.T, w_ada, b_ada.reshape(depth, 1, n))


def _ffn_kernel(x_ref, mod_ref, gain_ref, wg_ref, wu_ref, wd_ref, o_ref, *, mod_base):
    x = x_ref[...]
    shift = mod_ref[0, mod_base:mod_base + 1, :]
    scale = mod_ref[0, mod_base + 1:mod_base + 2, :]
    gate = mod_ref[0, mod_base + 2:mod_base + 3, :]
    h = _norm_mod(x, gain_ref[...], shift, scale).astype(_BF16)
    d_ff = wg_ref.shape[1]
    acc = jnp.zeros(x.shape, _F32)
    for c in range(d_ff // FF_CHUNK):
        cols = slice(c * FF_CHUNK, (c + 1) * FF_CHUNK)
        g = jnp.dot(h, wg_ref[:, cols], preferred_element_type=_F32)
        u = jnp.dot(h, wu_ref[:, cols], preferred_element_type=_F32)
        a = (g * _sigmoid(g)) * u
        acc = acc + jnp.dot(a.astype(_BF16), wd_ref[cols, :], preferred_element_type=_F32)
    o_ref[...] = x + (0.5 * gate) * acc


def _ffn(x2, mod_l, gain, wg, wu, wd, *, mod_base, seq):
    n_tok, d = x2.shape
    d_ff = wg.shape[1]
    tiles_per_seq = seq // TOKEN_TILE
    return pl.pallas_call(
        functools.partial(_ffn_kernel, mod_base=mod_base),
        grid=(n_tok // TOKEN_TILE,),
        in_specs=[
            pl.BlockSpec((TOKEN_TILE, d), lambda i: (i, 0)),
            pl.BlockSpec((1, N_MOD, d), lambda i: (i // tiles_per_seq, 0, 0)),
            _resident((1, d), lambda i: (0, 0)),
            _resident((d, d_ff), lambda i: (0, 0)),
            _resident((d, d_ff), lambda i: (0, 0)),
            _resident((d_ff, d), lambda i: (0, 0)),
        ],
        out_specs=pl.BlockSpec((TOKEN_TILE, d), lambda i: (i, 0)),
        out_shape=jax.ShapeDtypeStruct((n_tok, d), _F32),
        compiler_params=_params("arbitrary"),
        name="swiglu_ffn",
    )(x2, mod_l, gain, wg, wu, wd)


def _log_sigmoid(x):
    return -(jnp.maximum(-x, 0.0) + jnp.log1p(jnp.exp(-jnp.abs(x))))


def _split3(x):
    p0 = x.astype(_BF16)
    r = x - p0.astype(_F32)
    p1 = r.astype(_BF16)
    p2 = (r - p1.astype(_F32)).astype(_BF16)
    return p0, p1, p2


def _inproj_kernel(x_ref, mod_ref, gain_ref, wmain_ref, wt_ref, wf_ref, fb_ref, headsum_ref, qkg_ref, qcg_ref,
                   wpool_ref, pscale_ref, kpad_in, vpad_in,
                   qa_ref, ka_ref, va_ref, yb_ref, kc_ref, qct_ref, vct_ref, kaug_ref, crow_ref,
                   ext_ref, carry_ref, *, tiles_per_seq):
    del kpad_in, vpad_in
    tm = x_ref.shape[0]
    t = pl.program_id(0) % tiles_per_seq

    @pl.when(t == 0)
    def _():
        carry_ref[...] = jnp.zeros_like(carry_ref)
        ext_ref[0:POOL_HALO, :] = jnp.zeros((POOL_HALO, WIDTH), _F32)

    h = _norm_mod(x_ref[...], gain_ref[...], mod_ref[0, 3:4, :], mod_ref[0, 4:5, :]).astype(_BF16)

    def proj(k):
        return jnp.dot(h, wmain_ref[:, k * WIDTH:(k + 1) * WIDTH], preferred_element_type=_F32)

    def head_norm(y, row):
        ssum = jnp.dot((y * y).astype(_BF16), headsum_ref[...], preferred_element_type=_F32)
        return (y * lax.rsqrt(ssum * (1.0 / HEAD_DIM) + EPS)) * qkg_ref[row:row + 1, :]

    qa_ref[...] = head_norm(proj(0), 0).astype(_BF16)
    ka_ref[0] = head_norm(proj(1), 1).astype(_BF16)
    va_ref[0] = proj(2).astype(_BF16)
    kc_ref[...] = head_norm(proj(4), 2).astype(_BF16)

    qt = lax.dot_general(wt_ref[0], h, _NT, preferred_element_type=_F32)
    for hd in range(N_HEADS):
        rows = slice(hd * HEAD_DIM, (hd + 1) * HEAD_DIM)
        y = qt[rows, :]
        ms = jnp.mean(y * y, axis=0, keepdims=True)
        qct_ref[0, rows, :] = ((y * lax.rsqrt(ms + EPS)) * qcg_ref[rows, :]).astype(_BF16)
    vct_ref[0] = lax.dot_general(wt_ref[1], h, _NT, preferred_element_type=_F32).astype(_BF16)

    u = proj(3)
    ext_ref[POOL_HALO:POOL_HALO + tm, :] = u
    pos = t * tm + lax.broadcasted_iota(jnp.int32, (tm, 1), 0)
    for g, w in enumerate(POOL_WINDOWS):
        lanes = slice(g * POOL_GROUP_DIM, (g + 1) * POOL_GROUP_DIM)
        ug = u[:, lanes]
        win = ug
        for k in range(1, w):
            win = win + ext_ref[POOL_HALO - k:POOL_HALO - k + tm, lanes]
        count = jnp.minimum(pos + 1, w).astype(_F32)
        diff = win / count - ug
        y = jnp.dot(diff.astype(_BF16), wpool_ref[g], preferred_element_type=_F32)
        yb_ref[:, lanes] = (y * pscale_ref[:, lanes]).astype(_BF16)
    ext_ref[0:POOL_HALO, :] = ext_ref[tm:tm + POOL_HALO, :]

    f_lin = jnp.dot(h, wf_ref[...], preferred_element_type=_F32)
    logf = _log_sigmoid(f_lin + fb_ref[...])
    row = lax.broadcasted_iota(jnp.int32, (tm, tm), 0)
    col = lax.broadcasted_iota(jnp.int32, (tm, tm), 1)
    tri = (row >= col).astype(_BF16)
    cum = carry_ref[0:1, :]
    for piece in _split3(logf):
        cum = cum + jnp.dot(tri, piece, preferred_element_type=_F32)
    carry_ref[0:1, :] = cum[tm - 1:tm, :]
    cum2 = cum * LOG2E
    crow_ref[0] = cum2.T[0:N_HEADS, :]
    n0, n1, n2 = (piece.astype(_F32) for piece in _split3(-cum2))
    lane = lax.broadcasted_iota(jnp.int32, (tm, LANES), 1)
    tail = jnp.where(lane < 3 * N_HEADS + 3, 1.0, 0.0)
    kaug = jnp.where(lane < N_HEADS, n0, jnp.where(lane < 2 * N_HEADS, n1, jnp.where(lane < 3 * N_HEADS, n2, tail)))
    kaug_ref[...] = kaug.astype(_BF16)


def _inproj(x2, mod_l, gain, wmain, wt, wf, fb, headsum, qkg, qcg, wpool, pscale, *, bsz, seq):
    n_tok, d = x2.shape
    tm = TOKEN_TILE
    tps = seq // tm
    kpad = jnp.zeros((bsz, seq + LEFT, WIDTH), _BF16)
    vpad = jnp.zeros((bsz, seq + LEFT, WIDTH), _BF16)
    tok = pl.BlockSpec((tm, WIDTH), lambda i: (i, 0))
    padded = pl.BlockSpec((1, tm, WIDTH), lambda i: (i // tps, i % tps + LEFT // tm, 0))
    transposed = pl.BlockSpec((1, WIDTH, tm), lambda i: (i // tps, 0, i % tps))
    flat = jax.ShapeDtypeStruct((n_tok, WIDTH), _BF16)
    pad_shape = jax.ShapeDtypeStruct((bsz, seq + LEFT, WIDTH), _BF16)
    t_shape = jax.ShapeDtypeStruct((bsz, WIDTH, seq), _BF16)
    return pl.pallas_call(
        functools.partial(_inproj_kernel, tiles_per_seq=tps),
        grid=(n_tok // tm,),
        in_specs=[
            pl.BlockSpec((tm, d), lambda i: (i, 0)),
            pl.BlockSpec((1, N_MOD, d), lambda i: (i // tps, 0, 0)),
            _resident((1, d), lambda i: (0, 0)),
            _resident(wmain.shape, lambda i: (0, 0)),
            _resident(wt.shape, lambda i: (0, 0, 0)),
            _resident(wf.shape, lambda i: (0, 0)),
            _resident(fb.shape, lambda i: (0, 0)),
            _resident(headsum.shape, lambda i: (0, 0)),
            _resident(qkg.shape, lambda i: (0, 0)),
            _resident(qcg.shape, lambda i: (0, 0)),
            _resident(wpool.shape, lambda i: (0, 0, 0)),
            _resident(pscale.shape, lambda i: (0, 0)),
            pl.BlockSpec(memory_space=pl.ANY),
            pl.BlockSpec(memory_space=pl.ANY),
        ],
        out_specs=[tok, padded, padded, tok, tok, transposed, transposed,
                   pl.BlockSpec((tm, LANES), lambda i: (i, 0)),
                   pl.BlockSpec((1, N_HEADS, tm), lambda i: (i // tps, 0, i % tps))],
        out_shape=[flat, pad_shape, pad_shape, flat, flat, t_shape, t_shape,
                   jax.ShapeDtypeStruct((n_tok, LANES), _BF16),
                   jax.ShapeDtypeStruct((bsz, N_HEADS, seq), _F32)],
        scratch_shapes=[pltpu.VMEM((POOL_HALO + tm, WIDTH), _F32), pltpu.VMEM((8, LANES), _F32)],
        input_output_aliases={12: 1, 13: 2},
        compiler_params=_params("arbitrary"),
        name="mixer_inproj",
    )(x2, mod_l, gain, wmain, wt, wf, fb, headsum, qkg, qcg, wpool, pscale, kpad, vpad)


def _attn_a_kernel(q_ref, k_ref, v_ref, bias_ref, o_ref):
    r_rows = q_ref.shape[1]
    n_keys = bias_ref.shape[2]
    start = pl.multiple_of(pl.program_id(1) * r_rows, r_rows)
    lane = lax.broadcasted_iota(jnp.int32, (1, LANES), 1)
    low_half = lane < HEAD_DIM
    key_row = start + lax.broadcasted_iota(jnp.int32, (1, n_keys), 1)
    real = key_row >= LEFT
    for pair in range(N_HEADS // 2):
        lanes = slice(pair * LANES, (pair + 1) * LANES)
        q2 = q_ref[0, :, lanes]
        k2 = k_ref[0, pl.ds(start, n_keys), lanes]
        v2 = v_ref[0, pl.ds(start, n_keys), lanes]
        outs = []
        for hh in range(2):
            mine = low_half if hh == 0 else jnp.logical_not(low_half)
            qm = jnp.where(mine, q2, jnp.zeros_like(q2))
            s = lax.dot_general(qm, k2, _NT, preferred_element_type=_F32)
            s = jnp.where(real, s + bias_ref[2 * pair + hh], NEG_INF)
            m = jnp.max(s, axis=1, keepdims=True)
            p = jnp.exp(s - m)
            denom = jnp.sum(p, axis=1, keepdims=True)
            o = jnp.dot(p.astype(_BF16), v2, preferred_element_type=_F32)
            outs.append(o / denom)
        o_ref[0, :, lanes] = jnp.where(low_half, outs[0], outs[1]).astype(_BF16)


def _attn_a(qa3, kpad, vpad, bias):
    bsz, seq, _ = qa3.shape
    r_rows = ATTN_A_ROWS
    return pl.pallas_call(
        _attn_a_kernel,
        grid=(bsz, seq // r_rows),
        in_specs=[
            pl.BlockSpec((1, r_rows, WIDTH), lambda b, i: (b, i, 0)),
            pl.BlockSpec((1, seq + LEFT, WIDTH), lambda b, i: (b, 0, 0), pipeline_mode=pl.Buffered(1)),
            pl.BlockSpec((1, seq + LEFT, WIDTH), lambda b, i: (b, 0, 0), pipeline_mode=pl.Buffered(1)),
            _resident(bias.shape, lambda b, i: (0, 0, 0)),
        ],
        out_specs=pl.BlockSpec((1, r_rows, WIDTH), lambda b, i: (b, i, 0)),
        out_shape=jax.ShapeDtypeStruct((bsz, seq, WIDTH), _BF16),
        compiler_params=_params("arbitrary", "arbitrary"),
        name="chunk_attention",
    )(qa3, kpad, vpad, bias)


def _band_bias(rel_bias_l):
    n_heads = rel_bias_l.shape[0]
    rows, keys = ATTN_A_ROWS, ATTN_A_KEYS
    period = rows + keys + 1
    k = np.arange(period)
    offset = np.where(k < keys, k, k - period)
    idx = np.clip(LEFT - offset, -REL_CLIP, REL_CLIP) + REL_CLIP
    vec = rel_bias_l[:, idx]
    skew = jnp.tile(vec, (1, rows))[:, :rows * (period - 1)].reshape(n_heads, rows, period - 1)
    r = np.arange(rows)[:, None]
    p = np.arange(keys)[None, :]
    in_band = (p // CHUNK >= r // CHUNK) & (p // CHUNK <= r // CHUNK + LEFT_CHUNKS)
    return jnp.where(in_band[None], skew[:, :, :keys], NEG_INF)


def _fox_kernel(lo_ref, qt_ref, k_ref, kaug_ref, vt_ref, crow_ref, o_ref, qcat_ref, m_ref, l_ref, acc_ref, ot_ref):
    tq = qt_ref.shape[2]
    tk = FOX_TK
    b = pl.program_id(0)
    j = pl.program_id(1)
    i = pl.program_id(2)
    q_start = pl.multiple_of(i * tq, tq)
    sub = lax.broadcasted_iota(jnp.int32, (LANES, tq), 0)
    key_idx = lax.broadcasted_iota(jnp.int32, (tk, tq), 0)
    qry_idx = lax.broadcasted_iota(jnp.int32, (tk, tq), 1)
    for hh in range(2):
        head = 2 * j + hh
        mine = (sub >= hh * HEAD_DIM) & (sub < (hh + 1) * HEAD_DIM)
        qcat_ref[hh, 0:LANES, :] = jnp.where(mine, qt_ref[0].astype(_F32), 0.0).astype(_BF16)
        c0, c1, c2 = (piece.astype(_F32) for piece in _split3(crow_ref[0, pl.ds(head, 1), pl.ds(q_start, tq)]))
        picks = (sub == head) | (sub == N_HEADS + head) | (sub == 2 * N_HEADS + head)
        qaug = jnp.where(sub == 3 * N_HEADS, c0, jnp.where(sub == 3 * N_HEADS + 1, c1, jnp.where(
            sub == 3 * N_HEADS + 2, c2, jnp.where(picks, 1.0, 0.0))))
        qcat_ref[hh, LANES:2 * LANES, :] = qaug.astype(_BF16)
        m_ref[...] = jnp.full(m_ref.shape, NEG_INF, _F32)
        l_ref[...] = jnp.zeros(l_ref.shape, _F32)
        acc_ref[...] = jnp.zeros(acc_ref.shape, _F32)

        def block(k_start, diagonal, hh=hh):
            kcat = jnp.concatenate([k_ref[0, pl.ds(k_start, tk), :], kaug_ref[0, pl.ds(k_start, tk), :]], axis=1)
            s = jnp.dot(kcat, qcat_ref[hh], preferred_element_type=_F32)
            if diagonal:
                s = jnp.where(key_idx <= qry_idx, s, NEG_INF)
            m_prev = m_ref[...]
            m_new = jnp.maximum(m_prev, jnp.max(s, axis=0, keepdims=True))
            alpha = jnp.exp2(m_prev - m_new)
            p = jnp.exp2(s - m_new)
            l_ref[...] = alpha * l_ref[...] + jnp.sum(p, axis=0, keepdims=True)
            vt = vt_ref[0, hh * HEAD_DIM:(hh + 1) * HEAD_DIM, pl.ds(k_start, tk)]
            acc_ref[...] = alpha * acc_ref[...] + jnp.dot(vt, p.astype(_BF16), preferred_element_type=_F32)
            m_ref[...] = m_new

        def body(kk, carry, block=block):
            block(pl.multiple_of(kk * tk, tk), False)
            return carry

        lo = lo_ref[(b * N_HEADS + head) * pl.num_programs(2) + i]
        lax.fori_loop(lo, i, body, 0)
        block(q_start, True)
        ot_ref[hh * HEAD_DIM:(hh + 1) * HEAD_DIM, :] = acc_ref[...] / l_ref[...]
    o_ref[0] = ot_ref[...].T.astype(_BF16)


def _fox_first_block(crow, qk_bound):
    seq = crow.shape[2]
    c_first = crow[:, :, ::FOX_TQ]
    c_last = crow[:, :, FOX_TK - 1::FOX_TK]
    gap = (2.0 * qk_bound + 1.0) * LOG2E + c_first[:, :, :, None] - c_last[:, :, None, :]
    before = np.arange(seq // FOX_TK)[None, :] < np.arange(seq // FOX_TQ)[:, None]
    skippable = (gap < -FOX_SKIP_NATS * LOG2E) & before
    return jnp.sum(skippable, axis=-1).astype(jnp.int32).reshape(-1)


def _fox(lo, qct, kc3, kaug3, vct, crow):
    bsz, seq, _ = kc3.shape
    tq = FOX_TQ
    pairs = N_HEADS // 2
    grid_spec = pltpu.PrefetchScalarGridSpec(
        num_scalar_prefetch=1,
        grid=(bsz, pairs, seq // tq),
        in_specs=[
            pl.BlockSpec((1, LANES, tq), lambda b, j, i, lo: (b, j, i)),
            pl.BlockSpec((1, seq, LANES), lambda b, j, i, lo: (b, 0, j)),
            pl.BlockSpec((1, seq, LANES), lambda b, j, i, lo: (b, 0, 0)),
            pl.BlockSpec((1, LANES, seq), lambda b, j, i, lo: (b, j, 0)),
            pl.BlockSpec((1, N_HEADS, seq), lambda b, j, i, lo: (b, 0, 0)),
        ],
        out_specs=pl.BlockSpec((1, tq, LANES), lambda b, j, i, lo: (b, i, j)),
        scratch_shapes=[pltpu.VMEM((2, 2 * LANES, tq), _BF16), pltpu.VMEM((1, tq), _F32),
                        pltpu.VMEM((1, tq), _F32), pltpu.VMEM((HEAD_DIM, tq), _F32),
                        pltpu.VMEM((LANES, tq), _F32)],
    )
    return pl.pallas_call(
        _fox_kernel,
        grid_spec=grid_spec,
        out_shape=jax.ShapeDtypeStruct((bsz, seq, WIDTH), _BF16),
        compiler_params=_params("arbitrary", "arbitrary", "arbitrary"),
        name="forgetting_attention",
    )(lo, qct, kc3, kaug3, vct, crow)


def _merge_kernel(x_ref, mod_ref, gain_ref, ya_ref, yb_ref, yc_ref, wgate_ref, wbranch_ref, wout_ref, o_ref):
    x = x_ref[...]
    d = x.shape[1]
    h = _norm_mod(x, gain_ref[...], mod_ref[0, 3:4, :], mod_ref[0, 4:5, :]).astype(_BF16)
    merged = jnp.zeros(x.shape, _F32)
    for n, y_ref in enumerate((ya_ref, yb_ref, yc_ref)):
        g = jnp.dot(h, wgate_ref[:, n * d:(n + 1) * d], preferred_element_type=_F32)
        branch = jnp.dot(y_ref[...], wbranch_ref[n], preferred_element_type=_F32)
        merged = merged + _sigmoid(g) * branch
    y = jnp.dot(merged.astype(_BF16), wout_ref[...], preferred_element_type=_F32)
    o_ref[...] = x + mod_ref[0, 5:6, :] * y


def _merge(x2, mod_l, gain, ya, yb, yc, wgate, wbranch, wout, *, seq):
    n_tok, d = x2.shape
    tm = TOKEN_TILE
    tps = seq // tm
    tok = pl.BlockSpec((tm, WIDTH), lambda i: (i, 0))
    return pl.pallas_call(
        _merge_kernel,
        grid=(n_tok // tm,),
        in_specs=[
            pl.BlockSpec((tm, d), lambda i: (i, 0)),
            pl.BlockSpec((1, N_MOD, d), lambda i: (i // tps, 0, 0)),
            _resident((1, d), lambda i: (0, 0)),
            tok, tok, tok,
            _resident(wgate.shape, lambda i: (0, 0)),
            _resident(wbranch.shape, lambda i: (0, 0, 0)),
            _resident(wout.shape, lambda i: (0, 0)),
        ],
        out_specs=pl.BlockSpec((tm, d), lambda i: (i, 0)),
        out_shape=jax.ShapeDtypeStruct((n_tok, d), _F32),
        compiler_params=_params("arbitrary"),
        name="gated_merge",
    )(x2, mod_l, gain, ya, yb, yc, wgate, wbranch, wout)


def kernel(x, c, w_ada, b_ada, norm_gain, ffn_w_gate, ffn_w_up, ffn_w_down, w_in, qk_gain,
           rel_bias, forget_bias, w_pool, pool_scale, w_branch, w_out):
    bsz, seq, d = x.shape
    depth = w_ada.shape[0]
    assert seq % TOKEN_TILE == 0 and seq % FOX_TQ == 0 and seq % ATTN_A_ROWS == 0
    assert LEFT % TOKEN_TILE == 0 and LEFT % ATTN_A_ROWS == 0 and FOX_TQ == FOX_TK
    assert d % LANES == 0 and w_in.shape[2] == 7 * WIDTH + N_HEADS + N_BRANCHES * d
    n_tok = bsz * seq

    mod = _adaln(c, w_ada, b_ada).reshape(depth, bsz, N_MOD, d)

    head_id = np.arange(WIDTH) // HEAD_DIM
    headsum = jnp.asarray(head_id[:, None] == head_id[None, :], _BF16)
    main_cols = 7 * WIDTH

    x2 = x.reshape(n_tok, d)
    for l in range(depth):
        gains = norm_gain[l].reshape(3, 1, d)
        x2 = _ffn(x2, mod[l], gains[0], ffn_w_gate[l, 0].astype(_BF16), ffn_w_up[l, 0].astype(_BF16),
                  ffn_w_down[l, 0].astype(_BF16), mod_base=0, seq=seq)

        w_l = w_in[l]
        wmain = jnp.concatenate([w_l[:, :4 * WIDTH], w_l[:, 5 * WIDTH:6 * WIDTH]], axis=1).astype(_BF16)
        wt = jnp.stack([w_l[:, 4 * WIDTH:5 * WIDTH].T, w_l[:, 6 * WIDTH:7 * WIDTH].T]).astype(_BF16)
        w_forget = w_l[:, main_cols:main_cols + N_HEADS]
        wf = jnp.pad(jnp.tile(w_forget, (1, 3)), ((0, 0), (0, LANES - 3 * N_HEADS))).astype(_BF16)
        fb = jnp.pad(jnp.tile(forget_bias[l], 3), (0, LANES - 3 * N_HEADS)).reshape(1, LANES)
        wgate = w_l[:, main_cols + N_HEADS:].astype(_BF16)
        q_scale = HEAD_DIM ** -0.5
        qkg = jnp.stack([jnp.tile(qk_gain[l, 0], N_HEADS) * q_scale, jnp.tile(qk_gain[l, 1], N_HEADS),
                         jnp.tile(qk_gain[l, 3], N_HEADS)])
        qcg = (jnp.tile(qk_gain[l, 2], N_HEADS) * (q_scale * LOG2E)).reshape(WIDTH, 1)
        qa, kpad, vpad, yb, kc, qct, vct, kaug, crow = _inproj(
            x2, mod[l], gains[1], wmain, wt, wf, fb, headsum, qkg, qcg, w_pool[l].astype(_BF16),
            pool_scale[l].reshape(1, WIDTH), bsz=bsz, seq=seq)

        ya = _attn_a(qa.reshape(bsz, seq, WIDTH), kpad, vpad, _band_bias(rel_bias[l]))

        qk_bound = 1.02 * HEAD_DIM ** 0.5 * jnp.max(jnp.abs(qk_gain[l, 2])) * jnp.max(jnp.abs(qk_gain[l, 3]))
        lo = _fox_first_block(crow, qk_bound)
        yc = _fox(lo, qct, kc.reshape(bsz, seq, WIDTH), kaug.reshape(bsz, seq, LANES), vct, crow)

        x2 = _merge(x2, mod[l], gains[1], ya.reshape(n_tok, WIDTH), yb, yc.reshape(n_tok, WIDTH),
                    wgate, w_branch[l].astype(_BF16), w_out[l].astype(_BF16), seq=seq)

        x2 = _ffn(x2, mod[l], gains[2], ffn_w_gate[l, 1].astype(_BF16), ffn_w_up[l, 1].astype(_BF16),
                  ffn_w_down[l, 1].astype(_BF16), mod_base=6, seq=seq)
    return x2.reshape(bsz, seq, d)
```
